```python
import math
import jax, jax.numpy as jnp
from jax import lax
import numpy as np

D_MODEL = 2048
BATCH = 4
SEQ = 4096
DEPTH = 4

D_MIX = D_MODEL
ATT_HEADS = 8
ATT_HEAD_DIM = 128
D_ATT = ATT_HEADS * ATT_HEAD_DIM
ROT_DIM = ATT_HEAD_DIM // 4
ROPE_THETA = 500000.0
DILATED_PATTERNS = ((128, 1), (512, 4), (2048, 16))
D_SSM = D_MIX - D_ATT
SSM_HEAD_DIM = 64
SSM_HEADS = D_SSM // SSM_HEAD_DIM
SSM_GROUPS = 2
SSM_HEADS_PER_GROUP = SSM_HEADS // SSM_GROUPS
SSM_STATE = 128
D_CONV = 5
CONV_DIM = D_SSM + 2 * SSM_GROUPS * SSM_STATE
CHUNK = 128
D_IN_PROJ = 3 * D_ATT + D_SSM + CONV_DIM + 2 * SSM_HEADS
D_FF = ((8 * D_MODEL // 3 + 255) // 256) * 256
N_EXPERTS = 8
TOP_K = 2
ALPHA = (2 * DEPTH) ** 0.25
BETA = (8 * DEPTH) ** -0.25
LN_EPS = 1e-5
RMS_EPS = 1e-6
NEG_INF = -1e30

kernel_name = 'hybrid_dilated_attn_ssd_moe_encoder'


def layer_norm(x, g, b):
    xf = x.astype(jnp.float32)
    mu = jnp.mean(xf, -1, keepdims=True)
    var = jnp.mean(jnp.square(xf - mu), -1, keepdims=True)
    return ((xf - mu) * lax.rsqrt(var + LN_EPS) * g + b).astype(x.dtype)


def group_rms_norm(y, g, groups):
    yf = y.astype(jnp.float32).reshape(*y.shape[:-1], groups, -1)
    yf = yf * lax.rsqrt(jnp.mean(jnp.square(yf), -1, keepdims=True) + RMS_EPS)
    return (yf.reshape(y.shape) * g).astype(y.dtype)


def partial_rope(t, cos, sin):
    half = ROT_DIM // 2
    cos, sin = cos.astype(t.dtype), sin.astype(t.dtype)
    x1, x2 = t[..., :half], t[..., half:ROT_DIM]
    return jnp.concatenate([x1 * cos - x2 * sin, x2 * cos + x1 * sin, t[..., ROT_DIM:]], axis=-1)


def dilated_window_attention(q, k, v, window, dilation):
    bsz, s, h, dh = q.shape
    side = window // (2 * dilation)
    n_sub = s // dilation
    n_blk = -(-n_sub // side)
    pad = n_blk * side - n_sub

    def to_blocks(t, extra):
        t = t.reshape(bsz, n_sub, dilation, h, dh).transpose(0, 2, 1, 3, 4)
        t = t.reshape(bsz * dilation, n_sub, h, dh)
        t = jnp.pad(t, ((0, 0), (extra * side, pad + extra * side), (0, 0), (0, 0)))
        return t.reshape(bsz * dilation, n_blk + 2 * extra, side, h, dh)

    def band(t):
        return jnp.concatenate([t[:, :-2], t[:, 1:-1], t[:, 2:]], axis=2)

    qb = to_blocks(q, 0)
    kb = band(to_blocks(k, 1))
    vb = band(to_blocks(v, 1))
    kidx = jnp.arange(-side, (n_blk + 1) * side).reshape(n_blk + 2, side)
    kidx = jnp.concatenate([kidx[:-2], kidx[1:-1], kidx[2:]], axis=1)
    qidx = jnp.arange(n_blk * side).reshape(n_blk, side)
    kk = kidx[:, None, :]
    mask = (jnp.abs(kk - qidx[:, :, None]) <= side) & (kk >= 0) & (kk < n_sub)

    scores = jnp.einsum('nbqhd,nbkhd->nbhqk', qb, kb).astype(jnp.float32) * (dh ** -0.5)
    scores = jnp.where(mask[None, :, None], scores, NEG_INF)
    m = jnp.max(scores, -1, keepdims=True)
    p = jnp.exp(scores - m)
    den = jnp.sum(p, -1)
    o = jnp.einsum('nbhqk,nbkhd->nbqhd', p.astype(v.dtype), vb).astype(jnp.float32)
    o = o / jnp.swapaxes(den, 2, 3)[..., None]
    lse = jnp.swapaxes(m[..., 0] + jnp.log(den), 2, 3)

    def from_blocks(t):
        t = t.reshape(bsz, dilation, n_blk * side, *t.shape[3:])[:, :, :n_sub]
        return jnp.swapaxes(t, 1, 2).reshape(bsz, s, *t.shape[3:])

    return from_blocks(o), from_blocks(lse)


def segsum(a):
    t = a.shape[-1]
    rep = jnp.broadcast_to(a[..., :, None], a.shape + (t,))
    rep = jnp.where(jnp.tril(jnp.ones((t, t), bool), -1), rep, 0.0)
    cs = jnp.cumsum(rep, axis=-2)
    return jnp.where(jnp.tril(jnp.ones((t, t), bool)), cs, -jnp.inf)


def ssd_scan(xdt, adt, bm, cm):
    bsz, s, g, e, p = xdt.shape
    n = bm.shape[-1]
    nc = s // CHUNK
    xc = xdt.reshape(bsz, nc, CHUNK, g, e, p)
    bc = bm.reshape(bsz, nc, CHUNK, g, n)
    cc = cm.reshape(bsz, nc, CHUNK, g, n)
    ac = adt.reshape(bsz, nc, CHUNK, g, e).transpose(0, 3, 4, 1, 2)
    a_cs = jnp.cumsum(ac, axis=-1)
    decay = jnp.exp(segsum(ac))
    cb = jnp.einsum('bclgn,bcsgn->bgcls', cc, bc)
    y_diag = jnp.einsum('bgecls,bcsgep->bclgep', cb[:, :, None] * decay, xc)
    decay_states = jnp.exp(a_cs[..., -1:] - a_cs)
    states = jnp.einsum('bclgn,bgecl,bclgep->bcgepn', bc, decay_states, xc)
    states = jnp.concatenate([jnp.zeros_like(states[:, :1]), states], axis=1)
    chunk_tot = jnp.pad(a_cs[..., -1], ((0, 0), (0, 0), (0, 0), (1, 0)))
    decay_chunk = jnp.exp(segsum(chunk_tot))
    states = jnp.einsum('bgezc,bcgepn->bzgepn', decay_chunk, states)[:, :-1]
    y_off = jnp.einsum('bclgn,bcgepn,bgecl->bclgep', cc, states, jnp.exp(a_cs))
    return (y_diag + y_off).reshape(bsz, s, g, e, p)


def centred_depthwise_conv(u, w, b):
    y = lax.conv_general_dilated(u, w[:, None, :], window_strides=(1,),
                                 padding=[(D_CONV // 2, D_CONV // 2)],
                                 dimension_numbers=('NWC', 'WIO', 'NWC'),
                                 feature_group_count=u.shape[-1])
    return y + b


def hybrid_mixer(x, cos, sin, w_in, conv_w, conv_b, dt_bias, a_log, d_skip,
                 attn_norm_g, ssm_norm_g, w_out):
    bsz, s, _ = x.shape
    proj = x @ w_in
    q, k, v, z, xbc, dt_raw = jnp.split(
        proj, [D_ATT, 2 * D_ATT, 3 * D_ATT, 3 * D_ATT + D_SSM, 3 * D_ATT + D_SSM + CONV_DIM], axis=-1)

    q = partial_rope(q.reshape(bsz, s, ATT_HEADS, ATT_HEAD_DIM), cos, sin)
    k = partial_rope(k.reshape(bsz, s, ATT_HEADS, ATT_HEAD_DIM), cos, sin)
    v = v.reshape(bsz, s, ATT_HEADS, ATT_HEAD_DIM)
    outs, lses = [], []
    for window, dilation in DILATED_PATTERNS:
        o, l = dilated_window_attention(q, k, v, window, dilation)
        outs.append(o)
        lses.append(l)
    wts = jax.nn.softmax(jnp.stack(lses), axis=0)
    y_att = jnp.einsum('pbsh,pbshd->bshd', wts, jnp.stack(outs)).astype(x.dtype)
    y_att = group_rms_norm(y_att.reshape(bsz, s, D_ATT), attn_norm_g, 1)

    xbc = jax.nn.silu(centred_depthwise_conv(xbc, conv_w, conv_b)).astype(jnp.float32)
    xs, bm, cm = jnp.split(xbc, [D_SSM, D_SSM + SSM_GROUPS * SSM_STATE], axis=-1)
    xs = xs.reshape(bsz, s, SSM_GROUPS, SSM_HEADS_PER_GROUP, SSM_HEAD_DIM)
    bm = bm.reshape(bsz, s, SSM_GROUPS, SSM_STATE)
    cm = cm.reshape(bsz, s, SSM_GROUPS, SSM_STATE)
    dt = jax.nn.softplus(dt_raw.astype(jnp.float32).reshape(bsz, s, 2, SSM_GROUPS, SSM_HEADS_PER_GROUP)
                         + dt_bias.astype(jnp.float32).reshape(2, SSM_GROUPS, SSM_HEADS_PER_GROUP))
    a = -jnp.exp(a_log.astype(jnp.float32)).reshape(2, SSM_GROUPS, SSM_HEADS_PER_GROUP)
    dt_f, dt_b = dt[:, :, 0], dt[:, :, 1]
    y_f = ssd_scan(xs * dt_f[..., None], dt_f * a[0], bm, cm)
    flip = lambda t: jnp.flip(t, axis=1)
    y_b = flip(ssd_scan(flip(xs * dt_b[..., None]), flip(dt_b * a[1]), flip(bm), flip(cm)))
    d = d_skip.astype(jnp.float32).reshape(SSM_GROUPS, SSM_HEADS_PER_GROUP, 1)
    y_ssm = (y_f + y_b + d * xs).reshape(bsz, s, D_SSM).astype(x.dtype)
    y_ssm = group_rms_norm(y_ssm * jax.nn.silu(z), ssm_norm_g, SSM_GROUPS)

    return jnp.concatenate([y_att, y_ssm], axis=-1) @ w_out


def swiglu(x, w_in, w_down):
    gate, up = jnp.split(x @ w_in, 2, axis=-1)
    return (jax.nn.silu(gate) * up) @ w_down


def moe_swiglu(x, router_w, w_in, w_down):
    logits = (x @ router_w).astype(jnp.float32)
    top_val, top_idx = lax.top_k(logits, TOP_K)
    gates = jax.nn.softmax(top_val, axis=-1)
    combine = jnp.sum(jax.nn.one_hot(top_idx, N_EXPERTS, dtype=jnp.float32) * gates[..., None], axis=-2)
    out = jnp.zeros_like(x)
    for e in range(N_EXPERTS):
        out = out + combine[..., e:e + 1].astype(x.dtype) * swiglu(x, w_in[e], w_down[e])
    return out


def setup_inputs(seed: int = 0) -> dict:
    key = jax.random.key(seed)
    ks = jax.random.split(key, 24)
    f32 = jnp.float32
    nrm = lambda k, shape, scale: jax.random.normal(k, shape, f32) * scale
    n_dense, n_moe = (DEPTH + 1) // 2, DEPTH // 2
    dt0 = jnp.exp(jax.random.uniform(ks[4], (DEPTH, 2, SSM_HEADS), f32,
                                     minval=math.log(1e-3), maxval=math.log(1e-1)))
    return {
        'x': nrm(ks[0], (BATCH, SEQ, D_MODEL), 1.0),
        'positions': jnp.broadcast_to(jnp.arange(SEQ, dtype=jnp.int32), (BATCH, SEQ)),
        'w_in': nrm(ks[1], (DEPTH, D_MODEL, D_IN_PROJ), D_MODEL ** -0.5),
        'conv_w': nrm(ks[2], (DEPTH, D_CONV, CONV_DIM), D_CONV ** -0.5),
        'conv_b': nrm(ks[3], (DEPTH, CONV_DIM), 0.02),
        'dt_bias': dt0 + jnp.log(-jnp.expm1(-dt0)),
        'a_log': jnp.log(jax.random.uniform(ks[5], (DEPTH, 2, SSM_HEADS), f32, minval=1.0, maxval=16.0)),
        'd_skip': 1.0 + nrm(ks[6], (DEPTH, SSM_HEADS), 0.1),
        'attn_norm_g': 1.0 + nrm(ks[7], (DEPTH, D_ATT), 0.02),
        'ssm_norm_g': 1.0 + nrm(ks[8], (DEPTH, D_SSM), 0.02),
        'w_out': nrm(ks[9], (DEPTH, D_MIX, D_MODEL), BETA * D_MIX ** -0.5),
        'ln1_g': 1.0 + nrm(ks[10], (DEPTH, D_MODEL), 0.02),
        'ln1_b': nrm(ks[11], (DEPTH, D_MODEL), 0.02),
        'ffn_w_in': nrm(ks[12], (n_dense, D_MODEL, 2 * D_FF), D_MODEL ** -0.5),
        'ffn_w_down': nrm(ks[13], (n_dense, D_FF, D_MODEL), BETA * D_FF ** -0.5),
        'router_w': nrm(ks[14], (n_moe, D_MODEL, N_EXPERTS), D_MODEL ** -0.5),
        'expert_w_in': nrm(ks[15], (n_moe, N_EXPERTS, D_MODEL, 2 * D_FF), D_MODEL ** -0.5),
        'expert_w_down': nrm(ks[16], (n_moe, N_EXPERTS, D_FF, D_MODEL), BETA * D_FF ** -0.5),
        'ln2_g': 1.0 + nrm(ks[17], (DEPTH, D_MODEL), 0.02),
        'ln2_b': nrm(ks[18], (DEPTH, D_MODEL), 0.02),
    }


def reference(x, positions, w_in, conv_w, conv_b, dt_bias, a_log, d_skip, attn_norm_g,
              ssm_norm_g, w_out, ln1_g, ln1_b, ffn_w_in, ffn_w_down, router_w,
              expert_w_in, expert_w_down, ln2_g, ln2_b):
    inv_freq = ROPE_THETA ** (-jnp.arange(0, ROT_DIM, 2, dtype=jnp.float32) / ROT_DIM)
    ang = positions.astype(jnp.float32)[..., None] * inv_freq
    cos, sin = jnp.cos(ang)[:, :, None, :], jnp.sin(ang)[:, :, None, :]
    for l in range(DEPTH):
        mix = hybrid_mixer(x, cos, sin, w_in[l], conv_w[l], conv_b[l], dt_bias[l], a_log[l],
                           d_skip[l], attn_norm_g[l], ssm_norm_g[l], w_out[l])
        x = layer_norm(ALPHA * x + mix, ln1_g[l], ln1_b[l])
        if l % 2 == 0:
            ff = swiglu(x, ffn_w_in[l // 2], ffn_w_down[l // 2])
        else:
            ff = moe_swiglu(x, router_w[l // 2], expert_w_in[l // 2], expert_w_down[l // 2])
        x = layer_norm(ALPHA * x + ff, ln2_g[l], ln2_b[l])
    return x
```

```python
import functools
import math

import jax
import jax.numpy as jnp
from jax import lax
from jax.experimental import pallas as pl
from jax.experimental.pallas import tpu as pltpu

F32 = jnp.float32
BF16 = jnp.bfloat16

D_MODEL = 2048
DEPTH = 4
ATT_HEADS = 8
HEAD_DIM = 128
D_ATT = ATT_HEADS * HEAD_DIM
ROT_DIM = HEAD_DIM // 4
ROT_HALF = ROT_DIM // 2
ROPE_THETA = 500000.0
DILATIONS = (1, 4, 16)
SIDE = 64
D_SSM = 1024
SSM_HEAD_DIM = 64
SSM_HEADS = D_SSM // SSM_HEAD_DIM
SSM_GROUPS = 2
HEADS_PER_GROUP = SSM_HEADS // SSM_GROUPS
SSM_STATE = 128
D_CONV = 5
CONV_DIM = D_SSM + 2 * SSM_GROUPS * SSM_STATE
CHUNK = 128
D_MAIN = 3 * D_ATT + D_SSM + CONV_DIM
D_FF = 5632
N_EXPERTS = 8
TOP_K = 2
ALPHA = (2 * DEPTH) ** 0.25
LN_EPS = 1e-5
RMS_EPS = 1e-6
NEG_INF = -1e30

LANES = 128
SUBLANES = 8
VMEM_LIMIT = 56 * 1024 * 1024


def _cparams(*sem):
    return pltpu.CompilerParams(dimension_semantics=sem, vmem_limit_bytes=VMEM_LIMIT)


def _proj_kernel(x_ref, w_ref, cos_ref, sin_ref, o_ref, *, n_rope_tiles):
    acc = jnp.dot(x_ref[...], w_ref[...], preferred_element_type=F32)
    n = pl.program_id(0)

    @pl.when(n < n_rope_tiles)
    def _():
        c = cos_ref[...]
        s = sin_ref[...]
        lane = lax.broadcasted_iota(jnp.int32, c.shape, 1)
        for h in range(acc.shape[1] // HEAD_DIM):
            t = acc[:, h * HEAD_DIM:(h + 1) * HEAD_DIM]
            rot = jnp.where(lane < ROT_HALF,
                            pltpu.roll(t, HEAD_DIM - ROT_HALF, 1),
                            pltpu.roll(t, ROT_HALF, 1))
            o_ref[:, h * HEAD_DIM:(h + 1) * HEAD_DIM] = t * c + rot * s

    @pl.when(n >= n_rope_tiles)
    def _():
        o_ref[...] = acc


def _input_projection(xb, w_main, cosf, sinf, layer, *, tm=2048, tn=512):
    t = xb.shape[0]
    return pl.pallas_call(
        functools.partial(_proj_kernel, n_rope_tiles=2 * D_ATT // tn),
        grid=(D_MAIN // tn, t // tm),
        in_specs=[
            pl.BlockSpec((tm, D_MODEL), lambda n, m: (m, 0)),
            pl.BlockSpec((None, D_MODEL, tn), lambda n, m: (layer, 0, n)),
            pl.BlockSpec((tm, HEAD_DIM), lambda n, m: (m, 0)),
            pl.BlockSpec((tm, HEAD_DIM), lambda n, m: (m, 0)),
        ],
        out_specs=pl.BlockSpec((tm, tn), lambda n, m: (m, n)),
        out_shape=jax.ShapeDtypeStruct((t, D_MAIN), F32),
        compiler_params=_cparams("arbitrary", "arbitrary"),
        name="input_projection",
    )(xb, w_main, cosf, sinf)


def _dt_kernel(x_ref, w_ref, b_ref, o_ref):
    acc = jnp.dot(x_ref[...], w_ref[...], preferred_element_type=F32) + b_ref[...]
    o_ref[...] = jnp.maximum(acc, 0.0) + jnp.log1p(jnp.exp(-jnp.abs(acc)))


def _dt_projection(xb, w_dt, dt_bias_row, layer, *, tm=2048):
    t = xb.shape[0]
    return pl.pallas_call(
        _dt_kernel,
        grid=(t // tm,),
        in_specs=[
            pl.BlockSpec((tm, D_MODEL), lambda m: (m, 0)),
            pl.BlockSpec((None, D_MODEL, LANES), lambda m: (layer, 0, 0)),
            pl.BlockSpec((None, 1, LANES), lambda m: (layer, 0, 0)),
        ],
        out_specs=pl.BlockSpec((tm, LANES), lambda m: (m, 0)),
        out_shape=jax.ShapeDtypeStruct((t, LANES), F32),
        compiler_params=_cparams("arbitrary"),
        name="dt_projection",
    )(xb, w_dt, dt_bias_row)


def _attn_kernel(q_ref, k_ref, v_ref, o_ref, lse_ref, *, seq):
    bq = 2 * SIDE
    bk = 4 * SIDE
    scale = HEAD_DIM ** -0.5

    def body(i, carry):
        i0 = pl.multiple_of(i * bq, bq)
        ks = pl.multiple_of(jnp.clip(i0 - SIDE, 0, seq - bk), SIDE)
        q = q_ref[pl.ds(i0, bq), :].astype(BF16)
        k = k_ref[pl.ds(ks, bk), :].astype(BF16)
        v = v_ref[pl.ds(ks, bk), :].astype(BF16)
        s = lax.dot_general(q, k, (((1,), (1,)), ((), ())), preferred_element_type=F32) * scale
        qi = i0 + lax.broadcasted_iota(jnp.int32, (bq, bk), 0)
        ki = ks + lax.broadcasted_iota(jnp.int32, (bq, bk), 1)
        s = jnp.where(jnp.abs(qi - ki) <= SIDE, s, NEG_INF)
        m = jnp.max(s, axis=-1, keepdims=True)
        p = jnp.exp(s - m)
        l = jnp.sum(p, axis=-1, keepdims=True)
        o = jnp.dot(p.astype(BF16), v, preferred_element_type=F32)
        o_ref[pl.ds(i0, bq), :] = o / l
        lse_ref[pl.ds(i0, bq), :] = jnp.broadcast_to(m + jnp.log(l), (bq, HEAD_DIM))
        return carry

    lax.fori_loop(0, seq // bq, body, 0)


def _banded_attention(proj, batch, seq_full, dil):
    t = proj.shape[0]
    seq = seq_full // dil
    cb = D_MAIN // HEAD_DIM
    hb = D_ATT // HEAD_DIM
    pv = proj.reshape(batch, seq, dil * D_MAIN)
    blk = (None, seq, HEAD_DIM)
    o, lse = pl.pallas_call(
        functools.partial(_attn_kernel, seq=seq),
        grid=(batch, dil, ATT_HEADS),
        in_specs=[
            pl.BlockSpec(blk, lambda b, r, h: (b, 0, r * cb + h)),
            pl.BlockSpec(blk, lambda b, r, h: (b, 0, r * cb + hb + h)),
            pl.BlockSpec(blk, lambda b, r, h: (b, 0, r * cb + 2 * hb + h)),
        ],
        out_specs=[
            pl.BlockSpec(blk, lambda b, r, h: (b, 0, r * hb + h)),
            pl.BlockSpec(blk, lambda b, r, h: (b, 0, r * hb + h)),
        ],
        out_shape=[jax.ShapeDtypeStruct((batch, seq, dil * D_ATT), F32)] * 2,
        compiler_params=_cparams("arbitrary", "arbitrary", "arbitrary"),
        name=f"banded_attention_d{dil}",
    )(pv, pv, pv)
    return o.reshape(t, D_ATT), lse.reshape(t, D_ATT)


def _conv_kernel(u_ref, prev_ref, next_ref, w_ref, b_ref, o_ref, scr, *, tiles_per_seq):
    m = pl.program_id(0)
    tm = u_ref.shape[0]
    pos = m % tiles_per_seq
    has_prev = (pos != 0).astype(F32)
    has_next = (pos != tiles_per_seq - 1).astype(F32)
    scr[0:SUBLANES, :] = prev_ref[...] * has_prev
    scr[SUBLANES:SUBLANES + tm, :] = u_ref[...]
    scr[SUBLANES + tm:2 * SUBLANES + tm, :] = next_ref[...] * has_next
    acc = jnp.zeros(u_ref.shape, F32) + b_ref[...]
    for j in range(D_CONV):
        off = SUBLANES - D_CONV // 2 + j
        acc = acc + w_ref[j:j + 1, :] * scr[off:off + tm, :]
    o_ref[...] = acc / (1.0 + jnp.exp(-acc))


def _conv_silu(proj, conv_w, conv_b, layer, seq_full, *, tm=512, tc=512):
    t = proj.shape[0]
    col0 = (D_MAIN - CONV_DIM) // tc
    hb = tm // SUBLANES
    nhb = t // SUBLANES
    return pl.pallas_call(
        functools.partial(_conv_kernel, tiles_per_seq=seq_full // tm),
        grid=(t // tm, CONV_DIM // tc),
        in_specs=[
            pl.BlockSpec((tm, tc), lambda m, j: (m, col0 + j)),
            pl.BlockSpec((SUBLANES, tc), lambda m, j: (jnp.maximum(m * hb - 1, 0), col0 + j)),
            pl.BlockSpec((SUBLANES, tc), lambda m, j: (jnp.minimum((m + 1) * hb, nhb - 1), col0 + j)),
            pl.BlockSpec((None, D_CONV, tc), lambda m, j: (layer, 0, j)),
            pl.BlockSpec((None, 1, tc), lambda m, j: (layer, 0, j)),
        ],
        out_specs=pl.BlockSpec((tm, tc), lambda m, j: (m, j)),
        out_shape=jax.ShapeDtypeStruct((t, CONV_DIM), F32),
        scratch_shapes=[pltpu.VMEM((tm + 2 * SUBLANES, tc), F32)],
        compiler_params=_cparams("arbitrary", "arbitrary"),
        name="conv_silu",
    )(proj, proj, proj, conv_w, conv_b)


def _ssd_direction(x_ref, b_ref, c_ref, dt, a_row, state_ref, y_ref, rows_ref, *, reverse):
    ck = CHUNK
    row = lax.broadcasted_iota(jnp.int32, (ck, ck), 0)
    col = lax.broadcasted_iota(jnp.int32, (ck, ck), 1)
    keep = (col >= row) if reverse else (col <= row)
    tmat = jnp.where(keep, 1.0, 0.0).astype(F32)
    a = dt * a_row
    p = jnp.dot(tmat, a, preferred_element_type=F32, precision=lax.Precision.HIGHEST)
    tot = jnp.sum(a, axis=0, keepdims=True)
    dtw = dt * jnp.exp(tot - p)
    lane = lax.broadcasted_iota(jnp.int32, (ck, LANES), 1)
    packed = jnp.where(lane < SSM_HEADS, p,
                       jnp.where(lane < 2 * SSM_HEADS, pltpu.roll(dt, SSM_HEADS, 1),
                                 pltpu.roll(dtw, 2 * SSM_HEADS, 1)))
    rows_ref[...] = packed.T
    etot = jnp.exp(tot)
    half = lax.broadcasted_iota(jnp.int32, (ck, LANES), 1) < SSM_HEAD_DIM

    for g in range(SSM_GROUPS):
        bg = b_ref[:, g * SSM_STATE:(g + 1) * SSM_STATE].astype(BF16)
        cg = c_ref[:, g * SSM_STATE:(g + 1) * SSM_STATE]
        cb = lax.dot_general(cg.astype(BF16), bg, (((1,), (1,)), ((), ())),
                             preferred_element_type=F32)
        bgt = b_ref[:, g * SSM_STATE:(g + 1) * SSM_STATE].T
        for pr in range(HEADS_PER_GROUP // 2):
            e0 = g * HEADS_PER_GROUP + 2 * pr
            xp = x_ref[:, e0 * SSM_HEAD_DIM:(e0 + 2) * SSM_HEAD_DIM]
            x_lo = jnp.where(half, xp, 0.0).astype(BF16)
            x_hi = jnp.where(half, 0.0, xp).astype(BF16)
            st = state_ref[e0 // 2]
            st_lo = jnp.where(half, st, 0.0).astype(BF16)
            st_hi = jnp.where(half, 0.0, st).astype(BF16)
            lhs, bw = [], []
            for e in (e0, e0 + 1):
                pcol = jnp.broadcast_to(p[:, e:e + 1], (ck, ck))
                prow = rows_ref[e:e + 1, :]
                dtrow = rows_ref[SSM_HEADS + e:SSM_HEADS + e + 1, :]
                dtwrow = rows_ref[2 * SSM_HEADS + e:2 * SSM_HEADS + e + 1, :]
                decay = jnp.exp(jnp.where(keep, pcol - prow, NEG_INF))
                lhs.append((cb * decay * dtrow).astype(BF16))
                lhs.append((cg * jnp.exp(pcol)).astype(BF16))
                bw.append((bgt * dtwrow).astype(BF16))
            y = jnp.dot(jnp.concatenate(lhs, axis=1),
                        jnp.concatenate([x_lo, st_lo, x_hi, st_hi], axis=0),
                        preferred_element_type=F32)
            y_ref[:, e0 * SSM_HEAD_DIM:(e0 + 2) * SSM_HEAD_DIM] = y
            upd = jnp.dot(jnp.concatenate(bw, axis=1),
                          jnp.concatenate([x_lo, x_hi], axis=0),
                          preferred_element_type=F32)
            dec = jnp.where(half[:1], etot[:, e0:e0 + 1], etot[:, e0 + 1:e0 + 2])
            state_ref[e0 // 2] = st * dec + upd


def _ssd_kernel(xf_ref, bf_ref, cf_ref, dtf_ref, xb_ref, bb_ref, cb_ref, dtb_ref, alog_ref,
                yf_ref, yb_ref, sf_ref, sb_ref, rows_ref):
    @pl.when(pl.program_id(1) == 0)
    def _():
        sf_ref[...] = jnp.zeros(sf_ref.shape, F32)
        sb_ref[...] = jnp.zeros(sb_ref.shape, F32)

    a_all = -jnp.exp(alog_ref[...])
    _ssd_direction(xf_ref, bf_ref, cf_ref, dtf_ref[...], a_all, sf_ref, yf_ref, rows_ref,
                   reverse=False)
    shift = LANES - SSM_HEADS
    _ssd_direction(xb_ref, bb_ref, cb_ref, pltpu.roll(dtb_ref[...], shift, 1),
                   pltpu.roll(a_all, shift, 1), sb_ref, yb_ref, rows_ref, reverse=True)


def _ssd_scan(xbc, dt, alog_row, layer, batch, seq_full):
    t = xbc.shape[0]
    nc = seq_full // CHUNK
    bcol = D_SSM // (SSM_GROUPS * SSM_STATE)
    fwd = lambda b, c: b * nc + c
    bwd = lambda b, c: b * nc + nc - 1 - c

    def specs(rowfn):
        return [
            pl.BlockSpec((CHUNK, D_SSM), lambda b, c: (rowfn(b, c), 0)),
            pl.BlockSpec((CHUNK, SSM_GROUPS * SSM_STATE), lambda b, c: (rowfn(b, c), bcol)),
            pl.BlockSpec((CHUNK, SSM_GROUPS * SSM_STATE), lambda b, c: (rowfn(b, c), bcol + 1)),
            pl.BlockSpec((CHUNK, LANES), lambda b, c: (rowfn(b, c), 0)),
        ]

    return pl.pallas_call(
        _ssd_kernel,
        grid=(batch, nc),
        in_specs=specs(fwd) + specs(bwd) + [pl.BlockSpec((None, 1, LANES), lambda b, c: (layer, 0, 0))],
        out_specs=[
            pl.BlockSpec((CHUNK, D_SSM), lambda b, c: (fwd(b, c), 0)),
            pl.BlockSpec((CHUNK, D_SSM), lambda b, c: (bwd(b, c), 0)),
        ],
        out_shape=[jax.ShapeDtypeStruct((t, D_SSM), F32)] * 2,
        scratch_shapes=[
            pltpu.VMEM((SSM_HEADS // 2, SSM_STATE, LANES), F32),
            pltpu.VMEM((SSM_HEADS // 2, SSM_STATE, LANES), F32),
            pltpu.VMEM((CHUNK, LANES), F32),
        ],
        compiler_params=_cparams("arbitrary", "arbitrary"),
        name="ssd_scan",
    )(xbc, xbc, xbc, dt, xbc, xbc, xbc, dt, alog_row)


def _mix_kernel(o1, o2, o3, l1, l2, l3, yf, yb, xs, z, dsk, ga, gs, out_ref):
    la, lb, lc = l1[...], l2[...], l3[...]
    mx = jnp.maximum(jnp.maximum(la, lb), lc)
    wa, wb, wc = jnp.exp(la - mx), jnp.exp(lb - mx), jnp.exp(lc - mx)
    y = (wa * o1[...] + wb * o2[...] + wc * o3[...]) / (wa + wb + wc)
    y = y * lax.rsqrt(jnp.mean(y * y, axis=-1, keepdims=True) + RMS_EPS) * ga[...]
    out_ref[:, 0:D_ATT] = y.astype(out_ref.dtype)

    zz = z[...]
    s = (yf[...] + yb[...] + dsk[...] * xs[...]) * (zz / (1.0 + jnp.exp(-zz)))
    gw = D_SSM // SSM_GROUPS
    for g in range(SSM_GROUPS):
        sg = s[:, g * gw:(g + 1) * gw]
        sg = sg * lax.rsqrt(jnp.mean(sg * sg, axis=-1, keepdims=True) + RMS_EPS)
        out_ref[:, D_ATT + g * gw:D_ATT + (g + 1) * gw] = (
            sg * gs[:, g * gw:(g + 1) * gw]).astype(out_ref.dtype)


def _mixer_epilogue(att, yf, yb, xbc, proj, dskip_row, attn_g, ssm_g, layer, *, tm=256):
    t = yf.shape[0]
    row = pl.BlockSpec((tm, D_ATT), lambda m: (m, 0))
    par = pl.BlockSpec((None, 1, D_ATT), lambda m: (layer, 0, 0))
    (o1, l1), (o2, l2), (o3, l3) = att
    return pl.pallas_call(
        _mix_kernel,
        grid=(t // tm,),
        in_specs=[row] * 9 + [pl.BlockSpec((tm, D_SSM), lambda m: (m, 3 * D_ATT // D_SSM)), par, par, par],
        out_specs=pl.BlockSpec((tm, D_MODEL), lambda m: (m, 0)),
        out_shape=jax.ShapeDtypeStruct((t, D_MODEL), BF16),
        compiler_params=_cparams("arbitrary"),
        name="mixer_epilogue",
    )(o1, o2, o3, l1, l2, l3, yf, yb, xbc, proj, dskip_row, attn_g, ssm_g)


def _layer_norm_store(s, g_ref, b_ref, of_ref, ob_ref):
    mu = jnp.mean(s, axis=-1, keepdims=True)
    d = s - mu
    var = jnp.mean(d * d, axis=-1, keepdims=True)
    y = d * lax.rsqrt(var + LN_EPS) * g_ref[...] + b_ref[...]
    of_ref[...] = y
    ob_ref[...] = y.astype(BF16)


def _mm_ln_kernel(a_ref, w_ref, r_ref, g_ref, b_ref, of_ref, ob_ref):
    acc = jnp.dot(a_ref[...], w_ref[...], preferred_element_type=F32)
    _layer_norm_store(ALPHA * r_ref[...] + acc, g_ref, b_ref, of_ref, ob_ref)


def _matmul_residual_ln(a, w, resid, g, b, layer, w_layer, *, tm=256):
    t, k = a.shape
    par = pl.BlockSpec((None, 1, D_MODEL), lambda m: (layer, 0, 0))
    return pl.pallas_call(
        _mm_ln_kernel,
        grid=(t // tm,),
        in_specs=[
            pl.BlockSpec((tm, k), lambda m: (m, 0)),
            pl.BlockSpec((None, k, D_MODEL), lambda m: (w_layer, 0, 0), pipeline_mode=pl.Buffered(1)),
            pl.BlockSpec((tm, D_MODEL), lambda m: (m, 0)),
            par, par,
        ],
        out_specs=[pl.BlockSpec((tm, D_MODEL), lambda m: (m, 0))] * 2,
        out_shape=[jax.ShapeDtypeStruct((t, D_MODEL), F32), jax.ShapeDtypeStruct((t, D_MODEL), BF16)],
        compiler_params=_cparams("arbitrary"),
        name="matmul_residual_ln",
    )(a, w, resid, g, b)


def _gate_up_kernel(x_ref, wg_ref, wu_ref, o_ref):
    x = x_ref[...]
    g = jnp.dot(x, wg_ref[...], preferred_element_type=F32)
    u = jnp.dot(x, wu_ref[...], preferred_element_type=F32)
    o_ref[...] = (g / (1.0 + jnp.exp(-g)) * u).astype(o_ref.dtype)


def _ffn_gate_up(xb, w, w_layer, *, tm=2048, tn=512):
    t = xb.shape[0]
    nt = D_FF // tn
    return pl.pallas_call(
        _gate_up_kernel,
        grid=(nt, t // tm),
        in_specs=[
            pl.BlockSpec((tm, D_MODEL), lambda n, m: (m, 0)),
            pl.BlockSpec((None, D_MODEL, tn), lambda n, m: (w_layer, 0, n)),
            pl.BlockSpec((None, D_MODEL, tn), lambda n, m: (w_layer, 0, n + nt)),
        ],
        out_specs=pl.BlockSpec((tm, tn), lambda n, m: (m, n)),
        out_shape=jax.ShapeDtypeStruct((t, D_FF), BF16),
        compiler_params=_cparams("arbitrary", "arbitrary"),
        name="ffn_gate_up",
    )(xb, w, w)


def _router_kernel(x_ref, w_ref, o_ref):
    logits = jnp.dot(x_ref[...], w_ref[...], preferred_element_type=F32,
                     precision=lax.Precision.HIGHEST)
    lane = lax.broadcasted_iota(jnp.int32, logits.shape, 1)
    lanef = lane.astype(F32)
    logits = jnp.where(lane < N_EXPERTS, logits, -jnp.inf)
    m1 = jnp.max(logits, axis=-1, keepdims=True)
    i1 = jnp.min(jnp.where(logits == m1, lanef, float(LANES)), axis=-1, keepdims=True)
    rest = jnp.where(lanef == i1, -jnp.inf, logits)
    m2 = jnp.max(rest, axis=-1, keepdims=True)
    i2 = jnp.min(jnp.where(rest == m2, lanef, float(LANES)), axis=-1, keepdims=True)
    e2 = jnp.exp(m2 - m1)
    g1 = 1.0 / (1.0 + e2)
    g2 = e2 / (1.0 + e2)
    o_ref[...] = jnp.where(lane == 0, i1, jnp.where(lane == 1, i2, jnp.where(lane == 2, g1, g2)))


def _router(x, w_router, w_layer, *, tm=1024):
    t = x.shape[0]
    return pl.pallas_call(
        _router_kernel,
        grid=(t // tm,),
        in_specs=[
            pl.BlockSpec((tm, D_MODEL), lambda m: (m, 0)),
            pl.BlockSpec((None, D_MODEL, LANES), lambda m: (w_layer, 0, 0)),
        ],
        out_specs=pl.BlockSpec((tm, LANES), lambda m: (m, 0)),
        out_shape=jax.ShapeDtypeStruct((t, LANES), F32),
        compiler_params=_cparams("arbitrary"),
        name="moe_router",
    )(x, w_router)


def _gather_kernel(idx_ref, src_ref, o_ref, buf, sem):
    tm = o_ref.shape[0]

    def row_copy(i, src_row):
        return pltpu.make_async_copy(src_ref.at[src_row], buf.at[i], sem)

    def issue(i, carry):
        row_copy(i, idx_ref[0, i]).start()
        return carry

    lax.fori_loop(0, tm, issue, 0, unroll=8)

    def drain(i, carry):
        row_copy(i, 0).wait()
        return carry

    lax.fori_loop(0, tm, drain, 0, unroll=8)
    o_ref[...] = buf[:, 0, :]


def _gather_rows(src, idx, *, tm=256):
    n = idx.shape[0]
    t, d = src.shape
    return pl.pallas_call(
        _gather_kernel,
        grid=(n // tm,),
        in_specs=[
            pl.BlockSpec((None, 1, tm), lambda r: (r, 0, 0), memory_space=pltpu.SMEM),
            pl.BlockSpec(memory_space=pl.ANY),
        ],
        out_specs=pl.BlockSpec((tm, d), lambda r: (r, 0)),
        out_shape=jax.ShapeDtypeStruct((n, d), src.dtype),
        scratch_shapes=[pltpu.VMEM((tm, 1, d), src.dtype), pltpu.SemaphoreType.DMA(())],
        compiler_params=_cparams("arbitrary"),
        name="gather_rows",
    )(idx.reshape(n // tm, 1, tm), src.reshape(t, 1, d))


def _moe_up_kernel(te_ref, nu_ref, x_ref, wg_ref, wu_ref, o_ref, wgb, wub):
    r = pl.program_id(1)
    changed = jnp.logical_or(r == 0, te_ref[r] != te_ref[jnp.maximum(r - 1, 0)])

    @pl.when(changed)
    def _():
        wgb[...] = wg_ref[...].astype(BF16)
        wub[...] = wu_ref[...].astype(BF16)

    @pl.when(r < nu_ref[0])
    def _():
        x = x_ref[...]
        g = jnp.dot(x, wgb[...], preferred_element_type=F32)
        u = jnp.dot(x, wub[...], preferred_element_type=F32)
        o_ref[...] = (g / (1.0 + jnp.exp(-g)) * u).astype(o_ref.dtype)

    @pl.when(r >= nu_ref[0])
    def _():
        o_ref[...] = jnp.zeros(o_ref.shape, o_ref.dtype)


def _moe_gate_up(xs, w, w_layer, tile_expert, n_used, *, tm, tn=512):
    n = xs.shape[0]
    nt = D_FF // tn
    return pl.pallas_call(
        _moe_up_kernel,
        grid_spec=pltpu.PrefetchScalarGridSpec(
            num_scalar_prefetch=2,
            grid=(nt, n // tm),
            in_specs=[
                pl.BlockSpec((tm, D_MODEL), lambda j, r, te, nu: (r, 0)),
                pl.BlockSpec((None, None, D_MODEL, tn), lambda j, r, te, nu: (w_layer, te[r], 0, j)),
                pl.BlockSpec((None, None, D_MODEL, tn), lambda j, r, te, nu: (w_layer, te[r], 0, j + nt)),
            ],
            out_specs=pl.BlockSpec((tm, tn), lambda j, r, te, nu: (r, j)),
            scratch_shapes=[pltpu.VMEM((D_MODEL, tn), BF16), pltpu.VMEM((D_MODEL, tn), BF16)],
        ),
        out_shape=jax.ShapeDtypeStruct((n, D_FF), BF16),
        compiler_params=_cparams("arbitrary", "arbitrary"),
        name="moe_gate_up",
    )(tile_expert, n_used, xs, w, w)


def _moe_down_kernel(te_ref, nu_ref, h_ref, w_ref, o_ref, wb):
    r = pl.program_id(1)
    changed = jnp.logical_or(r == 0, te_ref[r] != te_ref[jnp.maximum(r - 1, 0)])

    @pl.when(changed)
    def _():
        wb[...] = w_ref[...].astype(BF16)

    @pl.when(r < nu_ref[0])
    def _():
        o_ref[...] = jnp.dot(h_ref[...], wb[...], preferred_element_type=F32)

    @pl.when(r >= nu_ref[0])
    def _():
        o_ref[...] = jnp.zeros(o_ref.shape, o_ref.dtype)


def _moe_down(h, w, w_layer, tile_expert, n_used, *, tm, tn=512):
    n = h.shape[0]
    return pl.pallas_call(
        _moe_down_kernel,
        grid_spec=pltpu.PrefetchScalarGridSpec(
            num_scalar_prefetch=2,
            grid=(D_MODEL // tn, n // tm),
            in_specs=[
                pl.BlockSpec((tm, D_FF), lambda j, r, te, nu: (r, 0)),
                pl.BlockSpec((None, None, D_FF, tn), lambda j, r, te, nu: (w_layer, te[r], 0, j)),
            ],
            out_specs=pl.BlockSpec((tm, tn), lambda j, r, te, nu: (r, j)),
            scratch_shapes=[pltpu.VMEM((D_FF, tn), BF16)],
        ),
        out_shape=jax.ShapeDtypeStruct((n, D_MODEL), F32),
        compiler_params=_cparams("arbitrary", "arbitrary"),
        name="moe_down",
    )(tile_expert, n_used, h, w)


def _combine_kernel(idx_ref, y_ref, rt_ref, r_ref, g_ref, b_ref, of_ref, ob_ref, buf, sem):
    tm = r_ref.shape[0]

    def row_copy(i, k, src_row):
        return pltpu.make_async_copy(y_ref.at[src_row], buf.at[k, i], sem)

    def issue(i, carry):
        for k in range(TOP_K):
            row_copy(i, k, idx_ref[0, TOP_K * i + k]).start()
        return carry

    lax.fori_loop(0, tm, issue, 0, unroll=4)

    def drain(i, carry):
        for k in range(TOP_K):
            row_copy(i, k, 0).wait()
        return carry

    lax.fori_loop(0, tm, drain, 0, unroll=4)

    rt = rt_ref[...]
    ff = rt[:, 2:3] * buf[0, :, 0, :] + rt[:, 3:4] * buf[1, :, 0, :]
    _layer_norm_store(ALPHA * r_ref[...] + ff, g_ref, b_ref, of_ref, ob_ref)


def _moe_combine_ln(y, dest, route, resid, g, b, layer, *, tm=256):
    t = resid.shape[0]
    par = pl.BlockSpec((None, 1, D_MODEL), lambda m: (layer, 0, 0))
    return pl.pallas_call(
        _combine_kernel,
        grid=(t // tm,),
        in_specs=[
            pl.BlockSpec((None, 1, TOP_K * tm), lambda m: (m, 0, 0), memory_space=pltpu.SMEM),
            pl.BlockSpec(memory_space=pl.ANY),
            pl.BlockSpec((tm, LANES), lambda m: (m, 0)),
            pl.BlockSpec((tm, D_MODEL), lambda m: (m, 0)),
            par, par,
        ],
        out_specs=[pl.BlockSpec((tm, D_MODEL), lambda m: (m, 0))] * 2,
        out_shape=[jax.ShapeDtypeStruct((t, D_MODEL), F32), jax.ShapeDtypeStruct((t, D_MODEL), BF16)],
        scratch_shapes=[pltpu.VMEM((TOP_K, tm, 1, D_MODEL), F32), pltpu.SemaphoreType.DMA(())],
        compiler_params=_cparams("arbitrary"),
        name="moe_combine_ln",
    )(dest.reshape(t // tm, 1, TOP_K * tm), y.reshape(y.shape[0], 1, D_MODEL), route, resid, g, b)


def _moe_plan(route, tm):
    t = route.shape[0]
    n_pairs = t * TOP_K
    n_rows = n_pairs + N_EXPERTS * tm
    n_tiles = n_rows // tm
    expert = route[:, :TOP_K].astype(jnp.int32).reshape(n_pairs)
    onehot = (expert[:, None] == jnp.arange(N_EXPERTS, dtype=jnp.int32)[None, :]).astype(jnp.int32)
    csum = jnp.cumsum(onehot, axis=0)
    rank = jnp.sum(csum * onehot, axis=1) - 1
    counts = csum[-1]
    padded = ((counts + tm - 1) // tm) * tm
    ends = jnp.cumsum(padded)
    starts = ends - padded
    dest = (starts[expert] + rank).astype(jnp.int32)
    src = jnp.zeros((n_rows,), jnp.int32).at[dest].set(jnp.arange(n_pairs, dtype=jnp.int32) // TOP_K)
    tile_start = jnp.arange(n_tiles, dtype=jnp.int32) * tm
    tile_expert = jnp.minimum(jnp.sum((tile_start[:, None] >= ends[None, :]).astype(jnp.int32), axis=1),
                              N_EXPERTS - 1).astype(jnp.int32)
    n_used = (ends[-1:] // tm).astype(jnp.int32)
    return src, dest, tile_expert, n_used


def kernel(x, positions, w_in, conv_w, conv_b, dt_bias, a_log, d_skip, attn_norm_g, ssm_norm_g,
           w_out, ln1_g, ln1_b, ffn_w_in, ffn_w_down, router_w, expert_w_in, expert_w_down,
           ln2_g, ln2_b):
    batch, seq, _ = x.shape
    t = batch * seq
    moe_tm = 512

    inv_freq = ROPE_THETA ** (-jnp.arange(0, ROT_DIM, 2, dtype=F32) / ROT_DIM)
    ang = positions.astype(F32).reshape(t, 1) * inv_freq
    cos, sin = jnp.cos(ang), jnp.sin(ang)
    cosf = jnp.concatenate([cos, cos, jnp.ones((t, HEAD_DIM - ROT_DIM), F32)], axis=1)
    sinf = jnp.concatenate([-sin, sin, jnp.zeros((t, HEAD_DIM - ROT_DIM), F32)], axis=1)

    w_main = w_in[:, :, :D_MAIN].astype(BF16)
    w_dt = jnp.pad(w_in[:, :, D_MAIN:], ((0, 0), (0, 0), (0, LANES - 2 * SSM_HEADS))).astype(BF16)
    pad_row = lambda v: jnp.pad(v.reshape(DEPTH, 1, 2 * SSM_HEADS), ((0, 0), (0, 0), (0, LANES - 2 * SSM_HEADS)))
    dt_bias_row = pad_row(dt_bias)
    alog_row = pad_row(a_log)
    dskip_row = jnp.repeat(d_skip, SSM_HEAD_DIM, axis=1).reshape(DEPTH, 1, D_SSM)
    conv_b3 = conv_b.reshape(DEPTH, 1, CONV_DIM)
    row3 = lambda v: v.reshape(v.shape[0], 1, v.shape[1])
    w_out_b = w_out.astype(BF16)
    ffn_in_b = ffn_w_in.astype(BF16)
    ffn_down_b = ffn_w_down.astype(BF16)
    router_pad = jnp.pad(router_w, ((0, 0), (0, 0), (0, LANES - N_EXPERTS)))
    attn_g3, ssm_g3 = row3(attn_norm_g), row3(ssm_norm_g)
    ln1g, ln1b, ln2g, ln2b = row3(ln1_g), row3(ln1_b), row3(ln2_g), row3(ln2_b)

    xf = x.reshape(t, D_MODEL)
    xb = xf.astype(BF16)
    for l in range(DEPTH):
        proj = _input_projection(xb, w_main, cosf, sinf, l)
        dt = _dt_projection(xb, w_dt, dt_bias_row, l)
        att = [_banded_attention(proj, batch, seq, d) for d in DILATIONS]
        xbc = _conv_silu(proj, conv_w, conv_b3, l, seq)
        yf, yb = _ssd_scan(xbc, dt, alog_row, l, batch, seq)
        mix = _mixer_epilogue(att, yf, yb, xbc, proj, dskip_row, attn_g3, ssm_g3, l)
        xf, xb = _matmul_residual_ln(mix, w_out_b, xf, ln1g, ln1b, l, l)
        if l % 2 == 0:
            act = _ffn_gate_up(xb, ffn_in_b, l // 2)
            xf, xb = _matmul_residual_ln(act, ffn_down_b, xf, ln2g, ln2b, l, l // 2)
        else:
            route = _router(xf, router_pad, l // 2)
            src, dest, tile_expert, n_used = _moe_plan(route, moe_tm)
            xs = _gather_rows(xb, src)
            h = _moe_gate_up(xs, expert_w_in, l // 2, tile_expert, n_used, tm=moe_tm)
            y = _moe_down(h, expert_w_down, l // 2, tile_expert, n_used, tm=moe_tm)
            xf, xb = _moe_combine_ln(y, dest, route, xf, ln2g, ln2b, l)
    return xf.reshape(batch, seq, D_MODEL)
```

```python
import functools
import math

import jax
import jax.numpy as jnp
from jax import lax
from jax.experimental import pallas as pl
from jax.experimental.pallas import tpu as pltpu

F32 = jnp.float32
BF16 = jnp.bfloat16

D_MODEL = 2048
DEPTH = 4
ATT_HEADS = 8
HEAD_DIM = 128
D_ATT = ATT_HEADS * HEAD_DIM
ROT_DIM = HEAD_DIM // 4
ROT_HALF = ROT_DIM // 2
ROPE_THETA = 500000.0
DILATIONS = (1, 4, 16)
SIDE = 64
D_SSM = 1024
SSM_HEAD_DIM = 64
SSM_HEADS = D_SSM // SSM_HEAD_DIM
SSM_GROUPS = 2
HEADS_PER_GROUP = SSM_HEADS // SSM_GROUPS
SSM_STATE = 128
D_CONV = 5
CONV_DIM = D_SSM + 2 * SSM_GROUPS * SSM_STATE
CHUNK = 128
D_MAIN = 3 * D_ATT + D_SSM + CONV_DIM
D_FF = 5632
N_EXPERTS = 8
TOP_K = 2
ALPHA = (2 * DEPTH) ** 0.25
LN_EPS = 1e-5
RMS_EPS = 1e-6
NEG_INF = -1e30

LANES = 128
SUBLANES = 8
VMEM_LIMIT = 56 * 1024 * 1024


def _cparams(*sem):
    return pltpu.CompilerParams(dimension_semantics=sem, vmem_limit_bytes=VMEM_LIMIT)


def _proj_kernel(x_ref, w_ref, cos_ref, sin_ref, o_ref, *, n_rope_tiles):
    acc = jnp.dot(x_ref[...], w_ref[...], preferred_element_type=F32)
    n = pl.program_id(0)

    @pl.when(n < n_rope_tiles)
    def _():
        c = cos_ref[...]
        s = sin_ref[...]
        lane = lax.broadcasted_iota(jnp.int32, c.shape, 1)
        for h in range(acc.shape[1] // HEAD_DIM):
            t = acc[:, h * HEAD_DIM:(h + 1) * HEAD_DIM]
            rot = jnp.where(lane < ROT_HALF,
                            pltpu.roll(t, HEAD_DIM - ROT_HALF, 1),
                            pltpu.roll(t, ROT_HALF, 1))
            o_ref[:, h * HEAD_DIM:(h + 1) * HEAD_DIM] = t * c + rot * s

    @pl.when(n >= n_rope_tiles)
    def _():
        o_ref[...] = acc


def _input_projection(xb, w_main, cosf, sinf, layer, *, tm=2048, tn=512):
    t = xb.shape[0]
    return pl.pallas_call(
        functools.partial(_proj_kernel, n_rope_tiles=2 * D_ATT // tn),
        grid=(D_MAIN // tn, t // tm),
        in_specs=[
            pl.BlockSpec((tm, D_MODEL), lambda n, m: (m, 0)),
            pl.BlockSpec((None, D_MODEL, tn), lambda n, m: (layer, 0, n)),
            pl.BlockSpec((tm, HEAD_DIM), lambda n, m: (m, 0)),
            pl.BlockSpec((tm, HEAD_DIM), lambda n, m: (m, 0)),
        ],
        out_specs=pl.BlockSpec((tm, tn), lambda n, m: (m, n)),
        out_shape=jax.ShapeDtypeStruct((t, D_MAIN), F32),
        compiler_params=_cparams("arbitrary", "arbitrary"),
        name="input_projection",
    )(xb, w_main, cosf, sinf)


def _dt_kernel(x_ref, w_ref, b_ref, o_ref):
    acc = jnp.dot(x_ref[...], w_ref[...], preferred_element_type=F32) + b_ref[...]
    o_ref[...] = jnp.maximum(acc, 0.0) + jnp.log1p(jnp.exp(-jnp.abs(acc)))


def _dt_projection(xb, w_dt, dt_bias_row, layer, *, tm=2048):
    t = xb.shape[0]
    return pl.pallas_call(
        _dt_kernel,
        grid=(t // tm,),
        in_specs=[
            pl.BlockSpec((tm, D_MODEL), lambda m: (m, 0)),
            pl.BlockSpec((None, D_MODEL, LANES), lambda m: (layer, 0, 0)),
            pl.BlockSpec((None, 1, LANES), lambda m: (layer, 0, 0)),
        ],
        out_specs=pl.BlockSpec((tm, LANES), lambda m: (m, 0)),
        out_shape=jax.ShapeDtypeStruct((t, LANES), F32),
        compiler_params=_cparams("arbitrary"),
        name="dt_projection",
    )(xb, w_dt, dt_bias_row)


ATT_BQ = 2 * SIDE
ATT_BK = 4 * SIDE


def _attn_kernel(q_ref, k_ref, v_ref, o_ref, m_ref, l_ref, qc, kc, vc, *, seq_full):
    scale = HEAD_DIM ** -0.5
    rel = (lax.broadcasted_iota(jnp.int32, (ATT_BQ, ATT_BK), 0)
           - lax.broadcasted_iota(jnp.int32, (ATT_BQ, ATT_BK), 1))

    for dil in DILATIONS:
        seq = seq_full // dil
        first = dil == DILATIONS[0]

        def rows(ref, start, size, dil=dil):
            if dil == 1:
                return ref.at[pl.ds(start, size), :]
            return ref.at[pl.ds(start, size, stride=dil), :]

        nblk = seq // ATT_BQ
        group = max(1, 4 // nblk)

        def block_body(r, base, i, dil=dil, seq=seq, first=first, rows=rows):
            if isinstance(i, int):
                i0 = i * ATT_BQ
                ks = min(max(i0 - SIDE, 0), seq - ATT_BK)
            else:
                i0 = pl.multiple_of(i * ATT_BQ, ATT_BQ)
                ks = pl.multiple_of(jnp.clip(i0 - SIDE, 0, seq - ATT_BK), SIDE)
            q = qc[pl.ds(base + i0, ATT_BQ), :]
            k = kc[pl.ds(base + ks, ATT_BK), :]
            v = vc[pl.ds(base + ks, ATT_BK), :]
            s = lax.dot_general(q, k, (((1,), (1,)), ((), ())), preferred_element_type=F32) * scale
            s = jnp.where(jnp.abs(rel + (i0 - ks)) <= SIDE, s, NEG_INF)
            mp = jnp.max(s, axis=-1, keepdims=True)
            p = jnp.exp(s - mp)
            lp = jnp.sum(p, axis=-1, keepdims=True)
            op = jnp.dot(p.astype(BF16), v, preferred_element_type=F32)
            dst = r + dil * i0
            if first:
                rows(o_ref, dst, ATT_BQ)[...] = op
                rows(m_ref, dst, ATT_BQ)[...] = jnp.broadcast_to(mp, (ATT_BQ, HEAD_DIM))
                rows(l_ref, dst, ATT_BQ)[...] = jnp.broadcast_to(lp, (ATT_BQ, HEAD_DIM))
            else:
                m_old = rows(m_ref, dst, ATT_BQ)[...]
                m_new = jnp.maximum(m_old, mp)
                a_old = jnp.exp(m_old - m_new)
                a_new = jnp.exp(mp - m_new)
                rows(l_ref, dst, ATT_BQ)[...] = a_old * rows(l_ref, dst, ATT_BQ)[...] + a_new * lp
                rows(o_ref, dst, ATT_BQ)[...] = a_old * rows(o_ref, dst, ATT_BQ)[...] + a_new * op
                rows(m_ref, dst, ATT_BQ)[...] = m_new

        def class_body(j, carry, dil=dil, seq=seq, rows=rows, nblk=nblk, group=group, block_body=block_body):
            for g in range(group):
                r = j * group + g
                base = g * seq
                qc[base:base + seq, :] = rows(q_ref, r, seq)[...].astype(BF16)
                kc[base:base + seq, :] = rows(k_ref, r, seq)[...].astype(BF16)
                vc[base:base + seq, :] = rows(v_ref, r, seq)[...].astype(BF16)
            if nblk * group <= 4:
                for g in range(group):
                    for i in range(nblk):
                        block_body(j * group + g, g * seq, i)
            else:
                def loop_body(i, c2):
                    block_body(j, 0, i)
                    return c2

                lax.fori_loop(0, nblk, loop_body, 0, unroll=4)
            return carry

        if dil == group:
            class_body(0, 0)
        else:
            lax.fori_loop(0, dil // group, class_body, 0)

    o_ref[...] = o_ref[...] / l_ref[...]


def _dilated_attention(proj, batch, seq_full):
    t = proj.shape[0]
    hb = D_ATT // HEAD_DIM
    blk = (seq_full, HEAD_DIM)
    return pl.pallas_call(
        functools.partial(_attn_kernel, seq_full=seq_full),
        grid=(batch, ATT_HEADS),
        in_specs=[
            pl.BlockSpec(blk, lambda b, h: (b, h)),
            pl.BlockSpec(blk, lambda b, h: (b, hb + h)),
            pl.BlockSpec(blk, lambda b, h: (b, 2 * hb + h)),
        ],
        out_specs=pl.BlockSpec(blk, lambda b, h: (b, h)),
        out_shape=jax.ShapeDtypeStruct((t, D_ATT), F32),
        scratch_shapes=[pltpu.VMEM(blk, F32), pltpu.VMEM(blk, F32),
                        pltpu.VMEM(blk, BF16), pltpu.VMEM(blk, BF16), pltpu.VMEM(blk, BF16)],
        compiler_params=_cparams("arbitrary", "arbitrary"),
        name="dilated_attention",
    )(proj, proj, proj)


def _conv_kernel(u_ref, prev_ref, next_ref, w_ref, b_ref, o_ref, scr, *, tiles_per_seq):
    m = pl.program_id(0)
    tm = u_ref.shape[0]
    pos = m % tiles_per_seq
    has_prev = (pos != 0).astype(F32)
    has_next = (pos != tiles_per_seq - 1).astype(F32)
    scr[0:SUBLANES, :] = prev_ref[...] * has_prev
    scr[SUBLANES:SUBLANES + tm, :] = u_ref[...]
    scr[SUBLANES + tm:2 * SUBLANES + tm, :] = next_ref[...] * has_next
    acc = jnp.zeros(u_ref.shape, F32) + b_ref[...]
    for j in range(D_CONV):
        off = SUBLANES - D_CONV // 2 + j
        acc = acc + w_ref[j:j + 1, :] * scr[off:off + tm, :]
    o_ref[...] = acc / (1.0 + jnp.exp(-acc))


def _conv_silu(proj, conv_w, conv_b, layer, seq_full, *, tm=512, tc=512):
    t = proj.shape[0]
    col0 = (D_MAIN - CONV_DIM) // tc
    hb = tm // SUBLANES
    nhb = t // SUBLANES
    return pl.pallas_call(
        functools.partial(_conv_kernel, tiles_per_seq=seq_full // tm),
        grid=(t // tm, CONV_DIM // tc),
        in_specs=[
            pl.BlockSpec((tm, tc), lambda m, j: (m, col0 + j)),
            pl.BlockSpec((SUBLANES, tc), lambda m, j: (jnp.maximum(m * hb - 1, 0), col0 + j)),
            pl.BlockSpec((SUBLANES, tc), lambda m, j: (jnp.minimum((m + 1) * hb, nhb - 1), col0 + j)),
            pl.BlockSpec((None, D_CONV, tc), lambda m, j: (layer, 0, j)),
            pl.BlockSpec((None, 1, tc), lambda m, j: (layer, 0, j)),
        ],
        out_specs=pl.BlockSpec((tm, tc), lambda m, j: (m, j)),
        out_shape=jax.ShapeDtypeStruct((t, CONV_DIM), F32),
        scratch_shapes=[pltpu.VMEM((tm + 2 * SUBLANES, tc), F32)],
        compiler_params=_cparams("arbitrary", "arbitrary"),
        name="conv_silu",
    )(proj, proj, proj, conv_w, conv_b)


def _ssd_direction(x_ref, b_ref, c_ref, dt, a_row, state_ref, y_ref, rows_ref, *, reverse):
    ck = CHUNK
    row = lax.broadcasted_iota(jnp.int32, (ck, ck), 0)
    col = lax.broadcasted_iota(jnp.int32, (ck, ck), 1)
    keep = (col >= row) if reverse else (col <= row)
    tmat = jnp.where(keep, 1.0, 0.0).astype(F32)
    a = dt * a_row
    p = jnp.dot(tmat, a, preferred_element_type=F32, precision=lax.Precision.HIGHEST)
    tot = jnp.sum(a, axis=0, keepdims=True)
    dtw = dt * jnp.exp(tot - p)
    lane = lax.broadcasted_iota(jnp.int32, (ck, LANES), 1)
    packed = jnp.where(lane < SSM_HEADS, p,
                       jnp.where(lane < 2 * SSM_HEADS, pltpu.roll(dt, SSM_HEADS, 1),
                                 pltpu.roll(dtw, 2 * SSM_HEADS, 1)))
    rows_ref[...] = packed.T
    etot = jnp.exp(tot)
    half = lax.broadcasted_iota(jnp.int32, (ck, LANES), 1) < SSM_HEAD_DIM

    for g in range(SSM_GROUPS):
        bg = b_ref[:, g * SSM_STATE:(g + 1) * SSM_STATE].astype(BF16)
        cg = c_ref[:, g * SSM_STATE:(g + 1) * SSM_STATE]
        cb = lax.dot_general(cg.astype(BF16), bg, (((1,), (1,)), ((), ())),
                             preferred_element_type=F32)
        bgt = b_ref[:, g * SSM_STATE:(g + 1) * SSM_STATE].T
        for pr in range(HEADS_PER_GROUP // 2):
            e0 = g * HEADS_PER_GROUP + 2 * pr
            xp = x_ref[:, e0 * SSM_HEAD_DIM:(e0 + 2) * SSM_HEAD_DIM]
            x_lo = jnp.where(half, xp, 0.0).astype(BF16)
            x_hi = jnp.where(half, 0.0, xp).astype(BF16)
            st = state_ref[e0 // 2]
            st_lo = jnp.where(half, st, 0.0).astype(BF16)
            st_hi = jnp.where(half, 0.0, st).astype(BF16)
            lhs, bw = [], []
            for e in (e0, e0 + 1):
                pcol = jnp.broadcast_to(p[:, e:e + 1], (ck, ck))
                prow = rows_ref[e:e + 1, :]
                dtrow = rows_ref[SSM_HEADS + e:SSM_HEADS + e + 1, :]
                dtwrow = rows_ref[2 * SSM_HEADS + e:2 * SSM_HEADS + e + 1, :]
                decay = jnp.exp(jnp.where(keep, pcol - prow, NEG_INF))
                lhs.append((cb * decay * dtrow).astype(BF16))
                lhs.append((cg * jnp.exp(pcol)).astype(BF16))
                bw.append((bgt * dtwrow).astype(BF16))
            y = jnp.dot(jnp.concatenate(lhs, axis=1),
                        jnp.concatenate([x_lo, st_lo, x_hi, st_hi], axis=0),
                        preferred_element_type=F32)
            y_ref[:, e0 * SSM_HEAD_DIM:(e0 + 2) * SSM_HEAD_DIM] = y
            upd = jnp.dot(jnp.concatenate(bw, axis=1),
                          jnp.concatenate([x_lo, x_hi], axis=0),
                          preferred_element_type=F32)
            dec = jnp.where(half[:1], etot[:, e0:e0 + 1], etot[:, e0 + 1:e0 + 2])
            state_ref[e0 // 2] = st * dec + upd


def _ssd_kernel(xf_ref, bf_ref, cf_ref, dtf_ref, xb_ref, bb_ref, cb_ref, dtb_ref, alog_ref,
                yf_ref, yb_ref, sf_ref, sb_ref, rows_f, rows_b):
    @pl.when(pl.program_id(1) == 0)
    def _():
        sf_ref[...] = jnp.zeros(sf_ref.shape, F32)
        sb_ref[...] = jnp.zeros(sb_ref.shape, F32)

    a_all = -jnp.exp(alog_ref[...])
    _ssd_direction(xf_ref, bf_ref, cf_ref, dtf_ref[...], a_all, sf_ref, yf_ref, rows_f,
                   reverse=False)
    shift = LANES - SSM_HEADS
    _ssd_direction(xb_ref, bb_ref, cb_ref, pltpu.roll(dtb_ref[...], shift, 1),
                   pltpu.roll(a_all, shift, 1), sb_ref, yb_ref, rows_b, reverse=True)


def _ssd_scan(xbc, dt, alog_row, layer, batch, seq_full):
    t = xbc.shape[0]
    nc = seq_full // CHUNK
    bcol = D_SSM // (SSM_GROUPS * SSM_STATE)
    fwd = lambda b, c: b * nc + c
    bwd = lambda b, c: b * nc + nc - 1 - c

    def specs(rowfn):
        return [
            pl.BlockSpec((CHUNK, D_SSM), lambda b, c: (rowfn(b, c), 0)),
            pl.BlockSpec((CHUNK, SSM_GROUPS * SSM_STATE), lambda b, c: (rowfn(b, c), bcol)),
            pl.BlockSpec((CHUNK, SSM_GROUPS * SSM_STATE), lambda b, c: (rowfn(b, c), bcol + 1)),
            pl.BlockSpec((CHUNK, LANES), lambda b, c: (rowfn(b, c), 0)),
        ]

    return pl.pallas_call(
        _ssd_kernel,
        grid=(batch, nc),
        in_specs=specs(fwd) + specs(bwd) + [pl.BlockSpec((None, 1, LANES), lambda b, c: (layer, 0, 0))],
        out_specs=[
            pl.BlockSpec((CHUNK, D_SSM), lambda b, c: (fwd(b, c), 0)),
            pl.BlockSpec((CHUNK, D_SSM), lambda b, c: (bwd(b, c), 0)),
        ],
        out_shape=[jax.ShapeDtypeStruct((t, D_SSM), F32)] * 2,
        scratch_shapes=[
            pltpu.VMEM((SSM_HEADS // 2, SSM_STATE, LANES), F32),
            pltpu.VMEM((SSM_HEADS // 2, SSM_STATE, LANES), F32),
            pltpu.VMEM((CHUNK, LANES), F32),
            pltpu.VMEM((CHUNK, LANES), F32),
        ],
        compiler_params=_cparams("arbitrary", "arbitrary"),
        name="ssd_scan",
    )(xbc, xbc, xbc, dt, xbc, xbc, xbc, dt, alog_row)


def _mix_kernel(ya, yf, yb, xs, z, dsk, ga, gs, out_ref):
    y = ya[...]
    y = y * lax.rsqrt(jnp.mean(y * y, axis=-1, keepdims=True) + RMS_EPS) * ga[...]
    out_ref[:, 0:D_ATT] = y.astype(out_ref.dtype)

    zz = z[...]
    s = (yf[...] + yb[...] + dsk[...] * xs[...]) * (zz / (1.0 + jnp.exp(-zz)))
    gw = D_SSM // SSM_GROUPS
    for g in range(SSM_GROUPS):
        sg = s[:, g * gw:(g + 1) * gw]
        sg = sg * lax.rsqrt(jnp.mean(sg * sg, axis=-1, keepdims=True) + RMS_EPS)
        out_ref[:, D_ATT + g * gw:D_ATT + (g + 1) * gw] = (
            sg * gs[:, g * gw:(g + 1) * gw]).astype(out_ref.dtype)


def _mixer_epilogue(y_att, yf, yb, xbc, proj, dskip_row, attn_g, ssm_g, layer, *, tm=512):
    t = yf.shape[0]
    row = pl.BlockSpec((tm, D_ATT), lambda m: (m, 0))
    par = pl.BlockSpec((None, 1, D_ATT), lambda m: (layer, 0, 0))
    return pl.pallas_call(
        _mix_kernel,
        grid=(t // tm,),
        in_specs=[row] * 4 + [pl.BlockSpec((tm, D_SSM), lambda m: (m, 3 * D_ATT // D_SSM)), par, par, par],
        out_specs=pl.BlockSpec((tm, D_MODEL), lambda m: (m, 0)),
        out_shape=jax.ShapeDtypeStruct((t, D_MODEL), BF16),
        compiler_params=_cparams("arbitrary"),
        name="mixer_epilogue",
    )(y_att, yf, yb, xbc, proj, dskip_row, attn_g, ssm_g)


def _layer_norm_store(s, g_ref, b_ref, of_ref, ob_ref):
    mu = jnp.mean(s, axis=-1, keepdims=True)
    d = s - mu
    var = jnp.mean(d * d, axis=-1, keepdims=True)
    y = d * lax.rsqrt(var + LN_EPS) * g_ref[...] + b_ref[...]
    of_ref[...] = y
    ob_ref[...] = y.astype(BF16)


def _mm_ln_kernel(a_ref, w_ref, r_ref, g_ref, b_ref, of_ref, ob_ref):
    acc = jnp.dot(a_ref[...], w_ref[...], preferred_element_type=F32)
    _layer_norm_store(ALPHA * r_ref[...] + acc, g_ref, b_ref, of_ref, ob_ref)


def _matmul_residual_ln(a, w, resid, g, b, layer, w_layer, *, tm=256):
    t, k = a.shape
    par = pl.BlockSpec((None, 1, D_MODEL), lambda m: (layer, 0, 0))
    return pl.pallas_call(
        _mm_ln_kernel,
        grid=(t // tm,),
        in_specs=[
            pl.BlockSpec((tm, k), lambda m: (m, 0)),
            pl.BlockSpec((None, k, D_MODEL), lambda m: (w_layer, 0, 0), pipeline_mode=pl.Buffered(1)),
            pl.BlockSpec((tm, D_MODEL), lambda m: (m, 0)),
            par, par,
        ],
        out_specs=[pl.BlockSpec((tm, D_MODEL), lambda m: (m, 0))] * 2,
        out_shape=[jax.ShapeDtypeStruct((t, D_MODEL), F32), jax.ShapeDtypeStruct((t, D_MODEL), BF16)],
        compiler_params=_cparams("arbitrary"),
        name="matmul_residual_ln",
    )(a, w, resid, g, b)


def _gate_up_kernel(x_ref, wg_ref, wu_ref, o_ref):
    x = x_ref[...]
    g = jnp.dot(x, wg_ref[...], preferred_element_type=F32)
    u = jnp.dot(x, wu_ref[...], preferred_element_type=F32)
    o_ref[...] = (g / (1.0 + jnp.exp(-g)) * u).astype(o_ref.dtype)


def _ffn_gate_up(xb, w, w_layer, *, tm=2048, tn=512):
    t = xb.shape[0]
    nt = D_FF // tn
    return pl.pallas_call(
        _gate_up_kernel,
        grid=(nt, t // tm),
        in_specs=[
            pl.BlockSpec((tm, D_MODEL), lambda n, m: (m, 0)),
            pl.BlockSpec((None, D_MODEL, tn), lambda n, m: (w_layer, 0, n)),
            pl.BlockSpec((None, D_MODEL, tn), lambda n, m: (w_layer, 0, n + nt)),
        ],
        out_specs=pl.BlockSpec((tm, tn), lambda n, m: (m, n)),
        out_shape=jax.ShapeDtypeStruct((t, D_FF), BF16),
        compiler_params=_cparams("arbitrary", "arbitrary"),
        name="ffn_gate_up",
    )(xb, w, w)


def _router_kernel(x_ref, w_ref, o_ref):
    logits = jnp.dot(x_ref[...], w_ref[...], preferred_element_type=F32,
                     precision=lax.Precision.HIGHEST)
    lane = lax.broadcasted_iota(jnp.int32, logits.shape, 1)
    lanef = lane.astype(F32)
    logits = jnp.where(lane < N_EXPERTS, logits, -jnp.inf)
    m1 = jnp.max(logits, axis=-1, keepdims=True)
    i1 = jnp.min(jnp.where(logits == m1, lanef, float(LANES)), axis=-1, keepdims=True)
    rest = jnp.where(lanef == i1, -jnp.inf, logits)
    m2 = jnp.max(rest, axis=-1, keepdims=True)
    i2 = jnp.min(jnp.where(rest == m2, lanef, float(LANES)), axis=-1, keepdims=True)
    e2 = jnp.exp(m2 - m1)
    g1 = 1.0 / (1.0 + e2)
    g2 = e2 / (1.0 + e2)
    o_ref[...] = jnp.where(lane == 0, i1, jnp.where(lane == 1, i2, jnp.where(lane == 2, g1, g2)))


def _router(x, w_router, w_layer, *, tm=1024):
    t = x.shape[0]
    return pl.pallas_call(
        _router_kernel,
        grid=(t // tm,),
        in_specs=[
            pl.BlockSpec((tm, D_MODEL), lambda m: (m, 0)),
            pl.BlockSpec((None, D_MODEL, LANES), lambda m: (w_layer, 0, 0)),
        ],
        out_specs=pl.BlockSpec((tm, LANES), lambda m: (m, 0)),
        out_shape=jax.ShapeDtypeStruct((t, LANES), F32),
        compiler_params=_cparams("arbitrary"),
        name="moe_router",
    )(x, w_router)


DMA_PRIORITIES = 2


def _issue_row_copies(n_rows, make_copy):
    def body(j, carry):
        for u in range(DMA_PRIORITIES):
            make_copy(DMA_PRIORITIES * j + u).start(priority=u)
        return carry

    lax.fori_loop(0, n_rows // DMA_PRIORITIES, body, 0, unroll=4)


def _wait_row_copies(n_rows, make_copy):
    def body(i, carry):
        make_copy(i).wait()
        return carry

    lax.fori_loop(0, n_rows, body, 0, unroll=8)


def _gather_kernel(idx_ref, idx_next_ref, src_ref, o_ref, buf, sem):
    tm = o_ref.shape[0]
    r = pl.program_id(0)
    slot = r % 2

    def row_copy(s, i, src_row):
        return pltpu.make_async_copy(src_ref.at[src_row], buf.at[s, i], sem.at[s])

    @pl.when(r == 0)
    def _():
        _issue_row_copies(tm, lambda i: row_copy(0, i, idx_ref[0, i]))

    @pl.when(r + 1 < pl.num_programs(0))
    def _():
        _issue_row_copies(tm, lambda i: row_copy(1 - slot, i, idx_next_ref[0, i]))

    _wait_row_copies(tm, lambda i: row_copy(slot, i, 0))
    o_ref[...] = buf[slot, :, 0, :]


def _gather_rows(src, idx, *, tm=256):
    n = idx.shape[0]
    t, d = src.shape
    nt = n // tm
    idx3 = idx.reshape(nt, 1, tm)
    return pl.pallas_call(
        _gather_kernel,
        grid=(nt,),
        in_specs=[
            pl.BlockSpec((None, 1, tm), lambda r: (r, 0, 0), memory_space=pltpu.SMEM),
            pl.BlockSpec((None, 1, tm), lambda r: (jnp.minimum(r + 1, nt - 1), 0, 0), memory_space=pltpu.SMEM),
            pl.BlockSpec(memory_space=pl.ANY),
        ],
        out_specs=pl.BlockSpec((tm, d), lambda r: (r, 0)),
        out_shape=jax.ShapeDtypeStruct((n, d), src.dtype),
        scratch_shapes=[pltpu.VMEM((2, tm, 1, d), src.dtype), pltpu.SemaphoreType.DMA((2,))],
        compiler_params=_cparams("arbitrary"),
        name="gather_rows",
    )(idx3, idx3, src.reshape(t, 1, d))


def _moe_up_kernel(te_ref, nu_ref, x_ref, wg_ref, wu_ref, o_ref, wgb, wub):
    r = pl.program_id(1)
    changed = jnp.logical_or(r == 0, te_ref[r] != te_ref[jnp.maximum(r - 1, 0)])

    @pl.when(changed)
    def _():
        wgb[...] = wg_ref[...].astype(BF16)
        wub[...] = wu_ref[...].astype(BF16)

    @pl.when(r < nu_ref[0])
    def _():
        x = x_ref[...]
        g = jnp.dot(x, wgb[...], preferred_element_type=F32)
        u = jnp.dot(x, wub[...], preferred_element_type=F32)
        o_ref[...] = (g / (1.0 + jnp.exp(-g)) * u).astype(o_ref.dtype)

    @pl.when(r >= nu_ref[0])
    def _():
        o_ref[...] = jnp.zeros(o_ref.shape, o_ref.dtype)


def _moe_gate_up(xs, w, w_layer, tile_expert, n_used, *, tm, tn=512):
    n = xs.shape[0]
    nt = D_FF // tn
    return pl.pallas_call(
        _moe_up_kernel,
        grid_spec=pltpu.PrefetchScalarGridSpec(
            num_scalar_prefetch=2,
            grid=(nt, n // tm),
            in_specs=[
                pl.BlockSpec((tm, D_MODEL), lambda j, r, te, nu: (r, 0)),
                pl.BlockSpec((None, None, D_MODEL, tn), lambda j, r, te, nu: (w_layer, te[r], 0, j)),
                pl.BlockSpec((None, None, D_MODEL, tn), lambda j, r, te, nu: (w_layer, te[r], 0, j + nt)),
            ],
            out_specs=pl.BlockSpec((tm, tn), lambda j, r, te, nu: (r, j)),
            scratch_shapes=[pltpu.VMEM((D_MODEL, tn), BF16), pltpu.VMEM((D_MODEL, tn), BF16)],
        ),
        out_shape=jax.ShapeDtypeStruct((n, D_FF), BF16),
        compiler_params=_cparams("arbitrary", "arbitrary"),
        name="moe_gate_up",
    )(tile_expert, n_used, xs, w, w)


def _moe_down_kernel(te_ref, nu_ref, h_ref, w_ref, o_ref, wb):
    r = pl.program_id(1)
    changed = jnp.logical_or(r == 0, te_ref[r] != te_ref[jnp.maximum(r - 1, 0)])

    @pl.when(changed)
    def _():
        wb[...] = w_ref[...].astype(BF16)

    @pl.when(r < nu_ref[0])
    def _():
        o_ref[...] = jnp.dot(h_ref[...], wb[...], preferred_element_type=F32)

    @pl.when(r >= nu_ref[0])
    def _():
        o_ref[...] = jnp.zeros(o_ref.shape, o_ref.dtype)


def _moe_down(h, w, w_layer, tile_expert, n_used, *, tm, tn=512):
    n = h.shape[0]
    return pl.pallas_call(
        _moe_down_kernel,
        grid_spec=pltpu.PrefetchScalarGridSpec(
            num_scalar_prefetch=2,
            grid=(D_MODEL // tn, n // tm),
            in_specs=[
                pl.BlockSpec((tm, D_FF), lambda j, r, te, nu: (r, 0)),
                pl.BlockSpec((None, None, D_FF, tn), lambda j, r, te, nu: (w_layer, te[r], 0, j)),
            ],
            out_specs=pl.BlockSpec((tm, tn), lambda j, r, te, nu: (r, j)),
            scratch_shapes=[pltpu.VMEM((D_FF, tn), BF16)],
        ),
        out_shape=jax.ShapeDtypeStruct((n, D_MODEL), F32),
        compiler_params=_cparams("arbitrary", "arbitrary"),
        name="moe_down",
    )(tile_expert, n_used, h, w)


def _combine_kernel(idx_ref, idx_next_ref, y_ref, rt_ref, r_ref, g_ref, b_ref, of_ref, ob_ref, buf, sem):
    tm = r_ref.shape[0]
    m = pl.program_id(0)
    slot = m % 2

    def row_copy(s, k, i, src_row):
        return pltpu.make_async_copy(y_ref.at[src_row], buf.at[s, k, i], sem.at[s])

    def issue_tile(s, idx):
        def body(i, carry):
            for k in range(TOP_K):
                row_copy(s, k, i, idx[0, TOP_K * i + k]).start(priority=k % DMA_PRIORITIES)
            return carry

        lax.fori_loop(0, tm, body, 0, unroll=4)

    @pl.when(m == 0)
    def _():
        issue_tile(0, idx_ref)

    @pl.when(m + 1 < pl.num_programs(0))
    def _():
        issue_tile(1 - slot, idx_next_ref)

    for k in range(TOP_K):
        _wait_row_copies(tm, lambda i, k=k: row_copy(slot, k, i, 0))

    rt = rt_ref[...]
    ff = rt[:, 2:3] * buf[slot, 0, :, 0, :] + rt[:, 3:4] * buf[slot, 1, :, 0, :]
    _layer_norm_store(ALPHA * r_ref[...] + ff, g_ref, b_ref, of_ref, ob_ref)


def _moe_combine_ln(y, dest, route, resid, g, b, layer, *, tm=256):
    t = resid.shape[0]
    nt = t // tm
    par = pl.BlockSpec((None, 1, D_MODEL), lambda m: (layer, 0, 0))
    idx3 = dest.reshape(nt, 1, TOP_K * tm)
    return pl.pallas_call(
        _combine_kernel,
        grid=(nt,),
        in_specs=[
            pl.BlockSpec((None, 1, TOP_K * tm), lambda m: (m, 0, 0), memory_space=pltpu.SMEM),
            pl.BlockSpec((None, 1, TOP_K * tm), lambda m: (jnp.minimum(m + 1, nt - 1), 0, 0),
                         memory_space=pltpu.SMEM),
            pl.BlockSpec(memory_space=pl.ANY),
            pl.BlockSpec((tm, LANES), lambda m: (m, 0)),
            pl.BlockSpec((tm, D_MODEL), lambda m: (m, 0)),
            par, par,
        ],
        out_specs=[pl.BlockSpec((tm, D_MODEL), lambda m: (m, 0))] * 2,
        out_shape=[jax.ShapeDtypeStruct((t, D_MODEL), F32), jax.ShapeDtypeStruct((t, D_MODEL), BF16)],
        scratch_shapes=[pltpu.VMEM((2, TOP_K, tm, 1, D_MODEL), F32), pltpu.SemaphoreType.DMA((2,))],
        compiler_params=_cparams("arbitrary"),
        name="moe_combine_ln",
    )(idx3, idx3, y.reshape(y.shape[0], 1, D_MODEL), route, resid, g, b)


def _moe_plan(route, tm):
    t = route.shape[0]
    n_pairs = t * TOP_K
    n_rows = n_pairs + N_EXPERTS * tm
    n_tiles = n_rows // tm
    expert = route[:, :TOP_K].astype(jnp.int32).reshape(n_pairs)
    onehot = (expert[:, None] == jnp.arange(N_EXPERTS, dtype=jnp.int32)[None, :]).astype(jnp.int32)
    csum = jnp.cumsum(onehot, axis=0)
    rank = jnp.sum(csum * onehot, axis=1) - 1
    counts = csum[-1]
    padded = ((counts + tm - 1) // tm) * tm
    ends = jnp.cumsum(padded)
    starts = ends - padded
    dest = (starts[expert] + rank).astype(jnp.int32)
    src = jnp.zeros((n_rows,), jnp.int32).at[dest].set(jnp.arange(n_pairs, dtype=jnp.int32) // TOP_K)
    tile_start = jnp.arange(n_tiles, dtype=jnp.int32) * tm
    tile_expert = jnp.minimum(jnp.sum((tile_start[:, None] >= ends[None, :]).astype(jnp.int32), axis=1),
                              N_EXPERTS - 1).astype(jnp.int32)
    n_used = (ends[-1:] // tm).astype(jnp.int32)
    return src, dest, tile_expert, n_used


def kernel(x, positions, w_in, conv_w, conv_b, dt_bias, a_log, d_skip, attn_norm_g, ssm_norm_g,
           w_out, ln1_g, ln1_b, ffn_w_in, ffn_w_down, router_w, expert_w_in, expert_w_down,
           ln2_g, ln2_b):
    batch, seq, _ = x.shape
    t = batch * seq
    moe_tm = 512

    inv_freq = ROPE_THETA ** (-jnp.arange(0, ROT_DIM, 2, dtype=F32) / ROT_DIM)
    ang = positions.astype(F32).reshape(t, 1) * inv_freq
    cos, sin = jnp.cos(ang), jnp.sin(ang)
    cosf = jnp.concatenate([cos, cos, jnp.ones((t, HEAD_DIM - ROT_DIM), F32)], axis=1)
    sinf = jnp.concatenate([-sin, sin, jnp.zeros((t, HEAD_DIM - ROT_DIM), F32)], axis=1)

    w_main = w_in[:, :, :D_MAIN].astype(BF16)
    w_dt = jnp.pad(w_in[:, :, D_MAIN:], ((0, 0), (0, 0), (0, LANES - 2 * SSM_HEADS))).astype(BF16)
    pad_row = lambda v: jnp.pad(v.reshape(DEPTH, 1, 2 * SSM_HEADS), ((0, 0), (0, 0), (0, LANES - 2 * SSM_HEADS)))
    dt_bias_row = pad_row(dt_bias)
    alog_row = pad_row(a_log)
    dskip_row = jnp.repeat(d_skip, SSM_HEAD_DIM, axis=1).reshape(DEPTH, 1, D_SSM)
    conv_b3 = conv_b.reshape(DEPTH, 1, CONV_DIM)
    row3 = lambda v: v.reshape(v.shape[0], 1, v.shape[1])
    w_out_b = w_out.astype(BF16)
    ffn_in_b = ffn_w_in.astype(BF16)
    ffn_down_b = ffn_w_down.astype(BF16)
    router_pad = jnp.pad(router_w, ((0, 0), (0, 0), (0, LANES - N_EXPERTS)))
    attn_g3, ssm_g3 = row3(attn_norm_g), row3(ssm_norm_g)
    ln1g, ln1b, ln2g, ln2b = row3(ln1_g), row3(ln1_b), row3(ln2_g), row3(ln2_b)

    xf = x.reshape(t, D_MODEL)
    xb = xf.astype(BF16)
    for l in range(DEPTH):
        proj = _input_projection(xb, w_main, cosf, sinf, l)
        dt = _dt_projection(xb, w_dt, dt_bias_row, l)
        y_att = _dilated_attention(proj, batch, seq)
        xbc = _conv_silu(proj, conv_w, conv_b3, l, seq)
        yf, yb = _ssd_scan(xbc, dt, alog_row, l, batch, seq)
        mix = _mixer_epilogue(y_att, yf, yb, xbc, proj, dskip_row, attn_g3, ssm_g3, l)
        xf, xb = _matmul_residual_ln(mix, w_out_b, xf, ln1g, ln1b, l, l)
        if l % 2 == 0:
            act = _ffn_gate_up(xb, ffn_in_b, l // 2)
            xf, xb = _matmul_residual_ln(act, ffn_down_b, xf, ln2g, ln2b, l, l // 2)
        else:
            route = _router(xf, router_pad, l // 2)
            src, dest, tile_expert, n_used = _moe_plan(route, moe_tm)
            xs = _gather_rows(xb, src)
            h = _moe_gate_up(xs, expert_w_in, l // 2, tile_expert, n_used, tm=moe_tm)
            y = _moe_down(h, expert_w_down, l // 2, tile_expert, n_used, tm=moe_tm)
            xf, xb = _moe_combine_ln(y, dest, route, xf, ln2g, ln2b, l)
    return xf.reshape(batch, seq, D_MODEL)
```
